```python
import math
import jax, jax.numpy as jnp
from jax import lax
import numpy as np

D_MODEL = 2048
BATCH = 1
SEQ = 16384
DEPTH = 2

N_MIXERS = 2
N_FOX_LAYERS = (DEPTH + 1) // 2
N_GDN_LAYERS = DEPTH // 2

FOX_HEAD_DIM = 128
FOX_HEADS = D_MODEL // FOX_HEAD_DIM
FOX_WIDTH = FOX_HEADS * FOX_HEAD_DIM
FOX_PROJ = 4 * FOX_WIDTH + FOX_HEADS
FOX_Q_BLOCK = 128

GDN_HEAD_K = 128
GDN_HEAD_V = 128
GDN_K_HEADS = D_MODEL // GDN_HEAD_K
GDN_V_HEADS = 2 * GDN_K_HEADS
GDN_KEY_DIM = GDN_K_HEADS * GDN_HEAD_K
GDN_VAL_DIM = GDN_V_HEADS * GDN_HEAD_V
GDN_CONV_DIM = 2 * GDN_KEY_DIM + GDN_VAL_DIM
GDN_PROJ = GDN_CONV_DIM + GDN_VAL_DIM + 2 * GDN_V_HEADS
GDN_CONV = 4
GDN_CHUNK = 64

N_EXPERTS = 32
N_GROUPS = 4
EXPERTS_PER_GROUP = N_EXPERTS // N_GROUPS
TOP_K = 2
D_EXPERT = D_MODEL // 4

N_MOD = 6
RMS_EPS = 1e-6
MASK_VALUE = -1e30

kernel_name = "hybrid_fox_gdn_grouped_moe_adaln"


def rmsnorm(x, g, eps=RMS_EPS):
    xf = x.astype(jnp.float32)
    y = xf * lax.rsqrt(jnp.mean(xf * xf, axis=-1, keepdims=True) + eps)
    return (y * g.astype(jnp.float32)).astype(x.dtype)


def l2norm(x, eps=1e-6):
    xf = x.astype(jnp.float32)
    return xf * lax.rsqrt(jnp.sum(xf * xf, axis=-1, keepdims=True) + eps)


def fox_attention(h, w_in, q_gain, k_gain, b_f, w_o):
    B, S, _ = h.shape
    H, Dh, Q = FOX_HEADS, FOX_HEAD_DIM, FOX_Q_BLOCK
    proj = h @ w_in
    q, k, v, o_gate, f_logit = jnp.split(
        proj, [FOX_WIDTH, 2 * FOX_WIDTH, 3 * FOX_WIDTH, 4 * FOX_WIDTH], axis=-1)
    q = rmsnorm(q.reshape(B, S, H, Dh), q_gain)
    k = rmsnorm(k.reshape(B, S, H, Dh), k_gain)
    v = v.reshape(B, S, H, Dh)
    log_f = jax.nn.log_sigmoid((f_logit + b_f).astype(jnp.float32))
    cum = jnp.cumsum(log_f, axis=1)
    cum_k = cum.transpose(0, 2, 1)
    n_blk = S // Q
    q_blocks = q.reshape(B, n_blk, Q, H, Dh).transpose(1, 0, 2, 3, 4)
    cq_blocks = cum.reshape(B, n_blk, Q, H).transpose(1, 0, 3, 2)
    kpos = jnp.arange(S)
    scale = Dh ** -0.5

    def block(args):
        i, qb, cqb = args
        s = jnp.einsum('bqhd,bkhd->bhqk', qb, k).astype(jnp.float32) * scale
        s = s + cqb[..., :, None] - cum_k[:, :, None, :]
        qpos = i * Q + jnp.arange(Q)
        s = jnp.where(kpos[None, :] <= qpos[:, None], s, MASK_VALUE)
        p = jax.nn.softmax(s, axis=-1).astype(v.dtype)
        return jnp.einsum('bhqk,bkhd->bqhd', p, v)

    out = lax.map(block, (jnp.arange(n_blk), q_blocks, cq_blocks))
    out = out.transpose(1, 0, 2, 3, 4).reshape(B, S, FOX_WIDTH)
    out = out * jax.nn.sigmoid(o_gate)
    return out @ w_o


def causal_depthwise_conv(x, w):
    K, C = w.shape
    return lax.conv_general_dilated(
        x, w[:, None, :], window_strides=(1,), padding=[(K - 1, 0)],
        dimension_numbers=('NWC', 'WIO', 'NWC'), feature_group_count=C)


def chunk_gated_delta_rule(q, k, v, g, beta):
    B, S, H, dk = q.shape
    C = GDN_CHUNK
    N = S // C
    q = q * (dk ** -0.5)

    def chunked(t):
        return t.reshape(B, N, C, H, -1).transpose(0, 3, 1, 2, 4)

    q, k, v = chunked(q), chunked(k), chunked(v)
    beta = beta.reshape(B, N, C, H).transpose(0, 3, 1, 2)
    g = jnp.cumsum(g.reshape(B, N, C, H).transpose(0, 3, 1, 2), axis=-1)
    causal = jnp.tril(jnp.ones((C, C), dtype=bool))
    strict = jnp.tril(jnp.ones((C, C), dtype=bool), -1)
    diff = g[..., :, None] - g[..., None, :]
    decay = jnp.where(causal, jnp.exp(jnp.where(causal, diff, 0.0)), 0.0)
    k_beta = k * beta[..., None]
    v_beta = v * beta[..., None]
    L = jnp.where(strict, jnp.einsum('bhncd,bhnsd->bhncs', k_beta, k) * decay, 0.0)
    eye = jnp.eye(C, dtype=jnp.float32)
    T = lax.linalg.triangular_solve(eye + L, jnp.broadcast_to(eye, L.shape),
                                    left_side=True, lower=True, unit_diagonal=True)
    u = T @ v_beta
    w = T @ (k_beta * jnp.exp(g)[..., None])
    intra = jnp.where(causal, jnp.einsum('bhncd,bhnsd->bhncs', q, k) * decay, 0.0)
    g_last = g[..., -1]
    k_tail = k * jnp.exp(g_last[..., None] - g)[..., None]
    q_dec = q * jnp.exp(g)[..., None]

    def step(state, xs):
        q_i, w_i, u_i, intra_i, kt_i, gl_i = xs
        v_new = u_i - w_i @ state
        o = q_i @ state + intra_i @ v_new
        state = state * jnp.exp(gl_i)[..., None, None] + jnp.einsum('bhcd,bhce->bhde', kt_i, v_new)
        return state, o

    xs = tuple(jnp.moveaxis(t, 2, 0) for t in (q_dec, w, u, intra, k_tail, g_last))
    state0 = jnp.zeros((B, H, dk, v.shape[-1]), jnp.float32)
    _, o = lax.scan(step, state0, xs)
    return o.transpose(1, 0, 3, 2, 4).reshape(B, S, H, -1)


def gated_deltanet(h, w_in, conv_w, a_log, dt_bias, o_gain, w_out):
    B, S, _ = h.shape
    proj = h @ w_in
    qkv, z, b, a = jnp.split(
        proj, [GDN_CONV_DIM, GDN_CONV_DIM + GDN_VAL_DIM, GDN_CONV_DIM + GDN_VAL_DIM + GDN_V_HEADS],
        axis=-1)
    qkv = jax.nn.silu(causal_depthwise_conv(qkv, conv_w))
    q, k, v = jnp.split(qkv, [GDN_KEY_DIM, 2 * GDN_KEY_DIM], axis=-1)
    rep = GDN_V_HEADS // GDN_K_HEADS
    q = jnp.repeat(l2norm(q.reshape(B, S, GDN_K_HEADS, GDN_HEAD_K)), rep, axis=2)
    k = jnp.repeat(l2norm(k.reshape(B, S, GDN_K_HEADS, GDN_HEAD_K)), rep, axis=2)
    v = v.reshape(B, S, GDN_V_HEADS, GDN_HEAD_V).astype(jnp.float32)
    beta = jax.nn.sigmoid(b.astype(jnp.float32))
    g = -jnp.exp(a_log.astype(jnp.float32)) * jax.nn.softplus(
        (a + dt_bias).astype(jnp.float32))
    o = chunk_gated_delta_rule(q, k, v, g, beta)
    o = rmsnorm(o, o_gain) * jax.nn.silu(z.reshape(B, S, GDN_V_HEADS, GDN_HEAD_V).astype(jnp.float32))
    return o.reshape(B, S, GDN_VAL_DIM).astype(h.dtype) @ w_out


def grouped_moe(h, w_router, router_bias, w_gate, w_up, w_down):
    B, S, D = h.shape
    t = h.reshape(B * S, D)
    T = t.shape[0]
    scores = jax.nn.sigmoid((t @ w_router).astype(jnp.float32))
    sel = scores + router_bias.astype(jnp.float32)
    grp_score = lax.top_k(sel.reshape(T, N_GROUPS, EXPERTS_PER_GROUP), 2)[0].sum(-1)
    best_group = jnp.argmax(grp_score, axis=-1)
    grp_mask = jnp.repeat(jax.nn.one_hot(best_group, N_GROUPS, dtype=jnp.bool_), EXPERTS_PER_GROUP, axis=-1)
    _, idx = lax.top_k(jnp.where(grp_mask, sel, -jnp.inf), TOP_K)
    wts = jnp.take_along_axis(scores, idx, axis=-1)
    wts = wts / jnp.sum(wts, axis=-1, keepdims=True)
    combine = jnp.einsum('tk,tke->te', wts, jax.nn.one_hot(idx, N_EXPERTS, dtype=jnp.float32))
    acc = jnp.zeros_like(t)
    for e in range(N_EXPERTS):
        y = (jax.nn.silu(t @ w_gate[e]) * (t @ w_up[e])) @ w_down[e]
        acc = acc + combine[:, e:e + 1].astype(y.dtype) * y
    return acc.reshape(B, S, D)


def setup_inputs(seed: int = 0) -> dict:
    key = jax.random.key(seed)
    ks = jax.random.split(key, 24)
    D = D_MODEL
    f32 = jnp.float32

    def nrm(k, shape, fan_in, scale=1.0):
        return jax.random.normal(k, shape, f32) * (scale * fan_in ** -0.5)

    def gain(k, shape):
        return 1.0 + 0.02 * jax.random.normal(k, shape, f32)

    dt = jnp.exp(jax.random.uniform(ks[10], (N_GDN_LAYERS, GDN_V_HEADS), f32,
                                    minval=math.log(1e-3), maxval=math.log(1e-1)))
    return {
        "x": jax.random.normal(ks[0], (BATCH, SEQ, D), f32),
        "c": jax.random.normal(ks[1], (BATCH, D), f32),
        "fox_w_in": nrm(ks[2], (N_FOX_LAYERS, D, FOX_PROJ), D),
        "fox_q_gain": gain(ks[3], (N_FOX_LAYERS, FOX_HEAD_DIM)),
        "fox_k_gain": gain(ks[4], (N_FOX_LAYERS, FOX_HEAD_DIM)),
        "fox_b_f": jax.random.uniform(ks[5], (N_FOX_LAYERS, FOX_HEADS), f32, minval=1.0, maxval=5.0),
        "fox_w_o": nrm(ks[6], (N_FOX_LAYERS, FOX_WIDTH, D), FOX_WIDTH),
        "gdn_w_in": nrm(ks[7], (N_GDN_LAYERS, D, GDN_PROJ), D),
        "gdn_conv_w": nrm(ks[8], (N_GDN_LAYERS, GDN_CONV, GDN_CONV_DIM), GDN_CONV),
        "gdn_a_log": jnp.log(jax.random.uniform(ks[9], (N_GDN_LAYERS, GDN_V_HEADS), f32, minval=1.0, maxval=16.0)),
        "gdn_dt_bias": dt + jnp.log(-jnp.expm1(-dt)),
        "gdn_o_gain": gain(ks[11], (N_GDN_LAYERS, GDN_HEAD_V)),
        "gdn_w_out": nrm(ks[12], (N_GDN_LAYERS, GDN_VAL_DIM, D), GDN_VAL_DIM),
        "ada_w": nrm(ks[13], (DEPTH, D, N_MOD * D), D, 0.5),
        "ada_b": 0.02 * jax.random.normal(ks[14], (DEPTH, N_MOD * D), f32),
        "norm_mix": gain(ks[15], (DEPTH, D)),
        "norm_ffn": gain(ks[16], (DEPTH, D)),
        "w_router": nrm(ks[17], (D, N_EXPERTS), D),
        "router_bias": 0.01 * jax.random.normal(ks[18], (N_EXPERTS,), f32),
        "moe_w_gate": nrm(ks[19], (DEPTH, N_EXPERTS, D, D_EXPERT), D),
        "moe_w_up": nrm(ks[20], (DEPTH, N_EXPERTS, D, D_EXPERT), D),
        "moe_w_down": nrm(ks[21], (DEPTH, N_EXPERTS, D_EXPERT, D), D_EXPERT),
        "norm_final": gain(ks[22], (D,)),
    }


def reference(x, c, fox_w_in, fox_q_gain, fox_k_gain, fox_b_f, fox_w_o,
              gdn_w_in, gdn_conv_w, gdn_a_log, gdn_dt_bias, gdn_o_gain, gdn_w_out,
              ada_w, ada_b, norm_mix, norm_ffn, w_router, router_bias,
              moe_w_gate, moe_w_up, moe_w_down, norm_final):
    mod_all = jnp.einsum('bd,lde->ble', jax.nn.silu(c), ada_w) + ada_b
    for i in range(DEPTH):
        shift_m, scale_m, gate_m, shift_f, scale_f, gate_f = jnp.split(mod_all[:, i, None, :], N_MOD, axis=-1)
        h = rmsnorm(x, norm_mix[i]) * (1.0 + scale_m) + shift_m
        j = i // N_MIXERS
        if i % N_MIXERS == 0:
            y = fox_attention(h, fox_w_in[j], fox_q_gain[j], fox_k_gain[j], fox_b_f[j], fox_w_o[j])
        else:
            y = gated_deltanet(h, gdn_w_in[j], gdn_conv_w[j], gdn_a_log[j], gdn_dt_bias[j],
                               gdn_o_gain[j], gdn_w_out[j])
        x = x + gate_m * y
        h = rmsnorm(x, norm_ffn[i]) * (1.0 + scale_f) + shift_f
        x = x + gate_f * grouped_moe(h, w_router, router_bias, moe_w_gate[i], moe_w_up[i], moe_w_down[i])
    return rmsnorm(x, norm_final)
```

```python
import functools

import jax
import jax.numpy as jnp
from jax import lax
from jax.experimental import pallas as pl
from jax.experimental.pallas import tpu as pltpu

F32 = jnp.float32
BF16 = jnp.bfloat16
HIGHEST = lax.Precision.HIGHEST

V7X_VMEM_LIMIT_BYTES = 56 * 1024 * 1024
LANES = 128

D_MODEL = 2048
HEAD_DIM = 128
FOX_HEADS = 16
GDN_K_HEADS = 16
GDN_V_HEADS = 32
GDN_KEY_DIM = GDN_K_HEADS * HEAD_DIM
GDN_VAL_DIM = GDN_V_HEADS * HEAD_DIM
GDN_CONV_DIM = 2 * GDN_KEY_DIM + GDN_VAL_DIM
GDN_CONV = 4
GDN_CHUNK = 64
GDN_CHUNK_SHIFT = GDN_CHUNK.bit_length() - 1
GDN_GROUP = 256
N_EXPERTS = 32
N_GROUPS = 4
EXPERTS_PER_GROUP = 8
D_EXPERT = 512
RMS_EPS = 1e-6
MASK_VALUE = -1e30

MOE_TILE = 256
FLASH_TILE = 1024


def _cparams(semantics):
    return pltpu.CompilerParams(dimension_semantics=semantics,
                                vmem_limit_bytes=V7X_VMEM_LIMIT_BYTES)


def _dot(a, b):
    return jnp.dot(a, b, preferred_element_type=F32)


def _dot_nt(a, b):
    return lax.dot_general(a, b, (((1,), (1,)), ((), ())), preferred_element_type=F32)


def _dot_f32(a, b):
    return jnp.dot(a, b, precision=HIGHEST, preferred_element_type=F32)


def _softplus(x):
    return jnp.maximum(x, 0.0) + jnp.log1p(jnp.exp(-jnp.abs(x)))


def _silu(x):
    return x * jax.nn.sigmoid(x)


def _mod_kernel(c_ref, w_ref, b_ref, o_ref):
    d = w_ref.shape[0]
    rows = 256

    def body(k, acc):
        sl = pl.ds(pl.multiple_of(k * rows, rows), rows)
        c = c_ref[sl, :]
        return acc + jnp.sum(w_ref[sl, :] * _silu(c), axis=0, keepdims=True)

    acc = lax.fori_loop(0, d // rows, body, jnp.zeros(o_ref.shape, F32))
    o_ref[...] = acc + b_ref[...]


def _modulation(c, ada_w, ada_b):
    depth, d, n = ada_w.shape
    tn = 1024
    out = pl.pallas_call(
        _mod_kernel,
        out_shape=jax.ShapeDtypeStruct((depth, 1, n), F32),
        grid=(depth, n // tn),
        in_specs=[pl.BlockSpec((d, 1), lambda l, j: (0, 0)),
                  pl.BlockSpec((None, d, tn), lambda l, j: (l, 0, j)),
                  pl.BlockSpec((None, 1, tn), lambda l, j: (l, 0, j))],
        out_specs=pl.BlockSpec((None, 1, tn), lambda l, j: (l, 0, j)),
        compiler_params=_cparams(("arbitrary", "arbitrary")),
        name="adaln_modulation",
    )(c.reshape(d, 1), ada_w, ada_b.reshape(depth, 1, n))
    return out.reshape(depth, n)


def _route(logits, rbias):
    tm = logits.shape[0]
    lane = lax.broadcasted_iota(jnp.int32, (tm, LANES), 1)
    lane_f = lane.astype(F32)
    scores = jax.nn.sigmoid(logits)
    sel = scores + rbias
    neg = -jnp.inf
    best = None
    for g in range(N_GROUPS):
        in_group = (lane >= g * EXPERTS_PER_GROUP) & (lane < (g + 1) * EXPERTS_PER_GROUP)
        gm = jnp.where(in_group, sel, neg)
        m1 = jnp.max(gm, axis=1, keepdims=True)
        i1 = jnp.min(jnp.where(gm == m1, lane_f, float(LANES)), axis=1, keepdims=True)
        gm2 = jnp.where(lane_f == i1, neg, gm)
        m2 = jnp.max(gm2, axis=1, keepdims=True)
        i2 = jnp.min(jnp.where(gm2 == m2, lane_f, float(LANES)), axis=1, keepdims=True)
        sc = m1 + m2
        if best is None:
            best = (sc, i1, i2)
        else:
            better = sc > best[0]
            best = (jnp.where(better, sc, best[0]),
                    jnp.where(better, i1, best[1]),
                    jnp.where(better, i2, best[2]))
    _, ia, ib = best
    sa = jnp.sum(jnp.where(lane_f == ia, scores, 0.0), axis=1, keepdims=True)
    sb = jnp.sum(jnp.where(lane_f == ib, scores, 0.0), axis=1, keepdims=True)
    den = sa + sb
    return jnp.where(lane == 0, ia,
                     jnp.where(lane == 1, ib,
                               jnp.where(lane == 2, sa / den,
                                         jnp.where(lane == 3, sb / den, 0.0))))


def _norm_kernel(*refs, modulate, side, route):
    it = iter(refs)
    x_ref = next(it)
    g_ref = next(it)
    sc_ref = next(it) if modulate else None
    sh_ref = next(it) if modulate else None
    ws_ref = next(it) if side else None
    rb_ref = next(it) if route else None
    o_ref = next(it)
    s_ref = next(it) if side else None

    x = x_ref[...]
    y = x * lax.rsqrt(jnp.mean(x * x, axis=-1, keepdims=True) + RMS_EPS)
    y = y * g_ref[...]
    if modulate:
        y = y * (1.0 + sc_ref[...]) + sh_ref[...]
    o_ref[...] = y.astype(o_ref.dtype)
    if side:
        s = _dot_f32(y, ws_ref[...])
        if route:
            s = _route(s, rb_ref[...])
        s_ref[...] = s


def _norm(x, gain, scale=None, shift=None, w_side=None, rbias=None, out_dtype=BF16):
    t, d = x.shape
    tm = 512
    modulate = scale is not None
    side = w_side is not None
    route = rbias is not None
    row = lambda v: v.reshape(1, -1)
    vec_spec = pl.BlockSpec((1, d), lambda i: (0, 0))
    args = [x, row(gain)]
    in_specs = [pl.BlockSpec((tm, d), lambda i: (i, 0)), vec_spec]
    if modulate:
        args += [row(scale), row(shift)]
        in_specs += [vec_spec, vec_spec]
    out_shape = [jax.ShapeDtypeStruct((t, d), out_dtype)]
    out_specs = [pl.BlockSpec((tm, d), lambda i: (i, 0))]
    if side:
        ns = w_side.shape[1]
        args.append(w_side)
        in_specs.append(pl.BlockSpec((d, ns), lambda i: (0, 0)))
        out_shape.append(jax.ShapeDtypeStruct((t, ns), F32))
        out_specs.append(pl.BlockSpec((tm, ns), lambda i: (i, 0)))
    if route:
        args.append(row(rbias))
        in_specs.append(pl.BlockSpec((1, LANES), lambda i: (0, 0)))
    out = pl.pallas_call(
        functools.partial(_norm_kernel, modulate=modulate, side=side, route=route),
        out_shape=out_shape,
        grid=(t // tm,),
        in_specs=in_specs,
        out_specs=out_specs,
        compiler_params=_cparams(("arbitrary",)),
        name="norm_modulate",
    )(*args)
    return out if side else out[0]


def _mm_kernel(a_ref, w_ref, o_ref):
    o_ref[...] = _dot(a_ref[...], w_ref[...].astype(BF16)).astype(o_ref.dtype)


def _mm_res_kernel(a_ref, w_ref, r_ref, g_ref, o_ref):
    o_ref[...] = r_ref[...] + g_ref[...] * _dot(a_ref[...], w_ref[...].astype(BF16))


def _matmul(a, w, layer, n_cols, out_dtype=BF16, resid=None, gate=None, tm=1024, tn=512):
    m, k = a.shape
    in_specs = [pl.BlockSpec((tm, k), lambda i, j: (i, 0)),
                pl.BlockSpec((None, k, tn), lambda i, j: (layer, 0, j))]
    args = [a, w]
    if resid is None:
        kern = _mm_kernel
    else:
        kern = _mm_res_kernel
        out_dtype = F32
        in_specs += [pl.BlockSpec((tm, tn), lambda i, j: (i, j)),
                     pl.BlockSpec((1, tn), lambda i, j: (0, j))]
        args += [resid, gate.reshape(1, -1)]
    return pl.pallas_call(
        kern,
        out_shape=jax.ShapeDtypeStruct((m, n_cols), out_dtype),
        grid=(m // tm, n_cols // tn),
        in_specs=in_specs,
        out_specs=pl.BlockSpec((tm, tn), lambda i, j: (i, j)),
        compiler_params=_cparams(("arbitrary", "arbitrary")),
        name="dense_matmul",
    )(*args)


def _fox_cum_kernel(fl_ref, bf_ref, o_ref, carry_ref):
    tm = fl_ref.shape[0]

    @pl.when(pl.program_id(0) == 0)
    def _():
        carry_ref[...] = jnp.zeros_like(carry_ref)

    log_f = -_softplus(-(fl_ref[...] + bf_ref[...]))
    r = lax.broadcasted_iota(jnp.int32, (tm, tm), 0)
    c = lax.broadcasted_iota(jnp.int32, (tm, tm), 1)
    cs = _dot_f32((c <= r).astype(F32), log_f) + carry_ref[...]
    o_ref[...] = cs
    carry_ref[...] = cs[tm - 1:tm, :]


def _fox_cumsum(f_logit, b_f_row):
    t = f_logit.shape[0]
    tm = 256
    return pl.pallas_call(
        _fox_cum_kernel,
        out_shape=jax.ShapeDtypeStruct((t, LANES), F32),
        grid=(t // tm,),
        in_specs=[pl.BlockSpec((tm, LANES), lambda i: (i, 0)),
                  pl.BlockSpec((1, LANES), lambda i: (0, 0))],
        out_specs=pl.BlockSpec((tm, LANES), lambda i: (i, 0)),
        scratch_shapes=[pltpu.VMEM((1, LANES), F32)],
        compiler_params=_cparams(("arbitrary",)),
        name="fox_forget_cumsum",
    )(f_logit, b_f_row)


def _split3(x):
    hi = x.astype(BF16).astype(F32)
    r1 = x - hi
    mid = r1.astype(BF16).astype(F32)
    return hi, mid, r1 - mid


def _fox_prep_kernel(q_ref, k_ref, cum_ref, qg_ref, kg_ref, qa_ref, ka_ref):
    h = pl.program_id(1)
    tm = q_ref.shape[0]
    lane = lax.broadcasted_iota(jnp.int32, (tm, LANES), 1)
    ccol = jnp.sum(jnp.where(lane == h, cum_ref[...], 0.0), axis=1, keepdims=True)
    hi, mid, lo = _split3(ccol)

    def nrm(ref, gain_ref):
        xf = ref[...].astype(F32)
        return xf * lax.rsqrt(jnp.mean(xf * xf, axis=-1, keepdims=True) + RMS_EPS) * gain_ref[...]

    qn = nrm(q_ref, qg_ref) * (HEAD_DIM ** -0.5)
    kn = nrm(k_ref, kg_ref)
    qx = jnp.where(lane == 0, hi, jnp.where(lane == 1, mid, jnp.where(lane == 2, lo,
                   jnp.where(lane < 6, 1.0, 0.0))))
    kx = jnp.where(lane < 3, 1.0, jnp.where(lane == 3, -hi, jnp.where(lane == 4, -mid,
                   jnp.where(lane == 5, -lo, 0.0))))
    qa_ref[:, :HEAD_DIM] = qn.astype(BF16)
    qa_ref[:, HEAD_DIM:] = qx.astype(BF16)
    ka_ref[:, :HEAD_DIM] = kn.astype(BF16)
    ka_ref[:, HEAD_DIM:] = kx.astype(BF16)


def _fox_prep(proj, cum, q_gain, k_gain):
    t = proj.shape[0]
    tm = 1024
    h_ = FOX_HEADS
    aug = jax.ShapeDtypeStruct((t, h_ * 2 * HEAD_DIM), BF16)
    return pl.pallas_call(
        _fox_prep_kernel,
        out_shape=[aug, aug],
        grid=(t // tm, h_),
        in_specs=[pl.BlockSpec((tm, HEAD_DIM), lambda i, h: (i, h)),
                  pl.BlockSpec((tm, HEAD_DIM), lambda i, h: (i, h_ + h)),
                  pl.BlockSpec((tm, LANES), lambda i, h: (i, 0)),
                  pl.BlockSpec((1, HEAD_DIM), lambda i, h: (0, 0)),
                  pl.BlockSpec((1, HEAD_DIM), lambda i, h: (0, 0))],
        out_specs=[pl.BlockSpec((tm, 2 * HEAD_DIM), lambda i, h: (i, h)),
                   pl.BlockSpec((tm, 2 * HEAD_DIM), lambda i, h: (i, h))],
        compiler_params=_cparams(("arbitrary", "arbitrary")),
        name="fox_qk_prep",
    )(proj, proj, cum, q_gain.reshape(1, -1), k_gain.reshape(1, -1))


def _flash_kernel(qb_ref, kb_ref, q_ref, k_ref, v_ref, og_ref, o_ref, m_ref, l_ref, acc_ref):
    p = pl.program_id(1)
    qb = qb_ref[p]
    kb = kb_ref[p]
    tq, tk = q_ref.shape[0], k_ref.shape[0]

    @pl.when(kb == 0)
    def _():
        m_ref[...] = jnp.full_like(m_ref, MASK_VALUE)
        l_ref[...] = jnp.zeros_like(l_ref)
        acc_ref[...] = jnp.zeros_like(acc_ref)

    def update(s):
        m_prev = m_ref[...]
        m_new = jnp.maximum(m_prev, jnp.max(s, axis=1, keepdims=True))
        alpha = jnp.exp(m_prev - m_new)
        pr = jnp.exp(s - m_new)
        l_ref[...] = alpha * l_ref[...] + jnp.sum(pr, axis=1, keepdims=True)
        acc_ref[...] = alpha * acc_ref[...] + _dot(pr.astype(BF16), v_ref[...])
        m_ref[...] = m_new

    @pl.when(kb < qb)
    def _():
        update(_dot_nt(q_ref[...], k_ref[...]))

    @pl.when(kb == qb)
    def _():
        s = _dot_nt(q_ref[...], k_ref[...])
        row = lax.broadcasted_iota(jnp.int32, (tq, tk), 0)
        col = lax.broadcasted_iota(jnp.int32, (tq, tk), 1)
        update(jnp.where(col <= row, s, MASK_VALUE))
        o = acc_ref[...] / l_ref[...] * jax.nn.sigmoid(og_ref[...].astype(F32))
        o_ref[...] = o.astype(o_ref.dtype)


def _fox_flash(q_aug, k_aug, proj):
    t = proj.shape[0]
    tile = FLASH_TILE
    nb = t // tile
    h_ = FOX_HEADS
    pairs = [(qb, kb) for qb in range(nb) for kb in range(qb + 1)]
    qb_arr = jnp.asarray([p[0] for p in pairs], jnp.int32)
    kb_arr = jnp.asarray([p[1] for p in pairs], jnp.int32)
    grid_spec = pltpu.PrefetchScalarGridSpec(
        num_scalar_prefetch=2,
        grid=(h_, len(pairs)),
        in_specs=[pl.BlockSpec((tile, 2 * HEAD_DIM), lambda h, p, qb, kb: (qb[p], h)),
                  pl.BlockSpec((tile, 2 * HEAD_DIM), lambda h, p, qb, kb: (kb[p], h)),
                  pl.BlockSpec((tile, HEAD_DIM), lambda h, p, qb, kb: (kb[p], 2 * h_ + h)),
                  pl.BlockSpec((tile, HEAD_DIM), lambda h, p, qb, kb: (qb[p], 3 * h_ + h))],
        out_specs=pl.BlockSpec((tile, HEAD_DIM), lambda h, p, qb, kb: (qb[p], h)),
        scratch_shapes=[pltpu.VMEM((tile, 1), F32), pltpu.VMEM((tile, 1), F32),
                        pltpu.VMEM((tile, HEAD_DIM), F32)])
    return pl.pallas_call(
        _flash_kernel,
        out_shape=jax.ShapeDtypeStruct((t, h_ * HEAD_DIM), BF16),
        grid_spec=grid_spec,
        compiler_params=_cparams(("arbitrary", "arbitrary")),
        name="fox_flash_attention",
    )(qb_arr, kb_arr, q_aug, k_aug, proj, proj)


def _fox_layer(x, mod, norm_gain, w_in, j, q_gain, k_gain, b_f, w_o):
    shift, scale, gate = mod[0], mod[1], mod[2]
    width = FOX_HEADS * HEAD_DIM
    w_tail = jnp.pad(w_in[j, :, 4 * width:], ((0, 0), (0, LANES - FOX_HEADS)))
    h, f_logit = _norm(x, norm_gain, scale, shift, w_side=w_tail)
    proj = _matmul(h, w_in, j, 4 * width)
    cum = _fox_cumsum(f_logit, jnp.pad(b_f, (0, LANES - FOX_HEADS)).reshape(1, LANES))
    q_aug, k_aug = _fox_prep(proj, cum, q_gain, k_gain)
    attn = _fox_flash(q_aug, k_aug, proj)
    return _matmul(attn, w_o, j, D_MODEL, resid=x, gate=gate)


def _gdn_conv_kernel(x_ref, halo_ref, w_ref, o_ref, *, l2, scale):
    i = pl.program_id(0)
    x = x_ref[...].astype(F32)
    tm, tc = x.shape
    hrows = halo_ref.shape[0]
    halo = jnp.where(i > 0, halo_ref[...].astype(F32), 0.0)
    full = jnp.concatenate([halo, x], axis=0)
    w = w_ref[...]
    y = w[GDN_CONV - 1:GDN_CONV, :] * x
    for j in range(1, GDN_CONV):
        shifted = pltpu.roll(full, j, 0)[hrows:, :]
        y = y + w[GDN_CONV - 1 - j:GDN_CONV - j, :] * shifted
    y = _silu(y)
    if l2:
        parts = []
        for hh in range(tc // HEAD_DIM):
            seg = y[:, hh * HEAD_DIM:(hh + 1) * HEAD_DIM]
            seg = seg * lax.rsqrt(jnp.sum(seg * seg, axis=-1, keepdims=True) + 1e-6)
            parts.append(seg * scale if scale != 1.0 else seg)
        y = jnp.concatenate(parts, axis=1)
    o_ref[...] = y.astype(o_ref.dtype)


def _gdn_conv(proj, conv_w, col0, ncols, l2, scale):
    t = proj.shape[0]
    tm, tc, hrows = 512, 512, 16
    cb0 = col0 // tc
    return pl.pallas_call(
        functools.partial(_gdn_conv_kernel, l2=l2, scale=scale),
        out_shape=jax.ShapeDtypeStruct((t, ncols), BF16),
        grid=(t // tm, ncols // tc),
        in_specs=[pl.BlockSpec((tm, tc), lambda i, j: (i, cb0 + j)),
                  pl.BlockSpec((hrows, tc),
                               lambda i, j: (jnp.maximum(i * (tm // hrows) - 1, 0), cb0 + j)),
                  pl.BlockSpec((GDN_CONV, tc), lambda i, j: (0, cb0 + j))],
        out_specs=pl.BlockSpec((tm, tc), lambda i, j: (i, j)),
        compiler_params=_cparams(("arbitrary", "arbitrary")),
        name="gdn_conv_silu",
    )(proj, proj, conv_w)


def _gdn_gates_kernel(b_ref, a_ref, alog_ref, dtb_ref,
                      beta_ref, gc_ref, egc_ref, et_ref, gcr_ref, eglr_ref):
    n = b_ref.shape[0]
    beta_ref[...] = jax.nn.sigmoid(b_ref[...])
    g = -jnp.exp(alog_ref[...]) * _softplus(a_ref[...] + dtb_ref[...])
    r = lax.broadcasted_iota(jnp.int32, (n, n), 0)
    c = lax.broadcasted_iota(jnp.int32, (n, n), 1)
    same = (r >> GDN_CHUNK_SHIFT) == (c >> GDN_CHUNK_SHIFT)
    gc = _dot_f32((same & (c <= r)).astype(F32), g)
    gl = _dot_f32(same.astype(F32), g)
    gc_ref[...] = gc
    egc_ref[...] = jnp.exp(gc)
    et_ref[...] = jnp.exp(gl - gc)
    gc_t = gc.T
    egl_t = jnp.exp(gl).T
    for hh in range(GDN_V_HEADS):
        gcr_ref[hh] = gc_t[hh:hh + 1, :]
        eglr_ref[hh] = egl_t[hh:hh + 1, :]


def _gdn_gates(tail, a_log, dt_bias):
    t = tail.shape[0]
    n = GDN_GROUP
    pad = lambda v: jnp.pad(v, (0, LANES - GDN_V_HEADS)).reshape(1, LANES)
    lane_major = jax.ShapeDtypeStruct((t, LANES), F32)
    row_major = jax.ShapeDtypeStruct((GDN_V_HEADS, 1, t), F32)
    blk = pl.BlockSpec((n, LANES), lambda i: (i, 0))
    rblk = pl.BlockSpec((GDN_V_HEADS, 1, n), lambda i: (0, 0, i))
    return pl.pallas_call(
        _gdn_gates_kernel,
        out_shape=[lane_major] * 4 + [row_major] * 2,
        grid=(t // n,),
        in_specs=[pl.BlockSpec((n, LANES), lambda i: (i, 0)),
                  pl.BlockSpec((n, LANES), lambda i: (i, 1)),
                  pl.BlockSpec((1, LANES), lambda i: (0, 0)),
                  pl.BlockSpec((1, LANES), lambda i: (0, 0))],
        out_specs=[blk] * 4 + [rblk] * 2,
        compiler_params=_cparams(("arbitrary",)),
        name="gdn_gates",
    )(tail, tail, pad(a_log), pad(dt_bias))


def _gdn_chunk_kernel(q_ref, k_ref, v_ref, z_ref, beta_ref, gc_ref, egc_ref, et_ref,
                      gcr_ref, eglr_ref, og_ref, o_ref, s_ref):
    hv = pl.program_id(0)
    n = GDN_GROUP
    cs = GDN_CHUNK

    @pl.when(pl.program_id(1) == 0)
    def _():
        s_ref[...] = jnp.zeros_like(s_ref)

    lane = lax.broadcasted_iota(jnp.int32, (n, LANES), 1)

    def col(ref):
        return jnp.sum(jnp.where(lane == hv, ref[...], 0.0), axis=1, keepdims=True)

    bcol, gcol, egcol, etcol = col(beta_ref), col(gc_ref), col(egc_ref), col(et_ref)
    grow = gcr_ref[...]
    eglrow = eglr_ref[...]
    q = q_ref[...].astype(F32)
    k = k_ref[...].astype(F32)
    v = v_ref[...].astype(F32)
    kb = k * bcol
    vb = v * bcol

    r = lax.broadcasted_iota(jnp.int32, (n, n), 0)
    c = lax.broadcasted_iota(jnp.int32, (n, n), 1)
    same = (r >> GDN_CHUNK_SHIFT) == (c >> GDN_CHUNK_SHIFT)
    causal = same & (c <= r)
    strict = same & (c < r)
    decay = jnp.where(causal, jnp.exp(jnp.where(causal, gcol - grow, 0.0)), 0.0)

    k16 = k_ref[...]
    lmat = jnp.where(strict, _dot_nt(kb.astype(BF16), k16) * decay, 0.0)
    pw = -lmat
    inv = jnp.where(r == c, 1.0, 0.0) + pw
    steps = GDN_CHUNK.bit_length() - 2
    for _ in range(steps):
        pw16 = pw.astype(BF16)
        pw = _dot(pw16, pw16)
        inv = inv + _dot(inv.astype(BF16), pw.astype(BF16))
    rhs = jnp.concatenate([vb, kb * egcol], axis=1).astype(BF16)
    uw = _dot(inv.astype(BF16), rhs)
    u = uw[:, :HEAD_DIM]
    w16 = uw[:, HEAD_DIM:].astype(BF16)
    intra = jnp.where(causal, _dot_nt(q_ref[...], k16) * decay, 0.0).astype(BF16)
    qd16 = (q * egcol).astype(BF16)
    kt_t = (k * etcol).T.astype(BF16)

    state = s_ref[...]
    v_new, o_state = [], []
    for i in range(n // cs):
        lo, hi = i * cs, (i + 1) * cs
        lhs = jnp.concatenate([w16[lo:hi], qd16[lo:hi]], axis=0)
        res = _dot(lhs, state.astype(BF16))
        vn = u[lo:hi] - res[:cs]
        o_state.append(res[cs:])
        v_new.append(vn)
        pieces = []
        if lo:
            pieces.append(jnp.zeros((lo, HEAD_DIM), F32))
        pieces.append(vn)
        if n - hi:
            pieces.append(jnp.zeros((n - hi, HEAD_DIM), F32))
        vn_pad = jnp.concatenate(pieces, axis=0).astype(BF16)
        state = state * eglrow[:, lo:lo + 1] + _dot(kt_t, vn_pad)
    s_ref[...] = state

    o = jnp.concatenate(o_state, axis=0) + _dot(intra, jnp.concatenate(v_new, axis=0).astype(BF16))
    on = o * lax.rsqrt(jnp.mean(o * o, axis=-1, keepdims=True) + RMS_EPS) * og_ref[...]
    o_ref[...] = (on * _silu(z_ref[...].astype(F32))).astype(o_ref.dtype)


def _gdn_chunk(qn, kn, vv, proj, gates, o_gain):
    t = proj.shape[0]
    n = GDN_GROUP
    beta, gc, egc, et, gcr, eglr = gates
    rep = GDN_V_HEADS // GDN_K_HEADS
    z_cb0 = GDN_CONV_DIM // HEAD_DIM
    head_blk = lambda f: pl.BlockSpec((n, HEAD_DIM), f)
    lanes_blk = pl.BlockSpec((n, LANES), lambda h, g: (g, 0))
    row_blk = pl.BlockSpec((None, 1, n), lambda h, g: (h, 0, g))
    return pl.pallas_call(
        _gdn_chunk_kernel,
        out_shape=jax.ShapeDtypeStruct((t, GDN_VAL_DIM), BF16),
        grid=(GDN_V_HEADS, t // n),
        in_specs=[head_blk(lambda h, g: (g, h // rep)),
                  head_blk(lambda h, g: (g, h // rep)),
                  head_blk(lambda h, g: (g, h)),
                  head_blk(lambda h, g: (g, z_cb0 + h)),
                  lanes_blk, lanes_blk, lanes_blk, lanes_blk,
                  row_blk, row_blk,
                  pl.BlockSpec((1, HEAD_DIM), lambda h, g: (0, 0))],
        out_specs=head_blk(lambda h, g: (g, h)),
        scratch_shapes=[pltpu.VMEM((HEAD_DIM, HEAD_DIM), F32)],
        compiler_params=_cparams(("arbitrary", "arbitrary")),
        name="gdn_chunk_delta_rule",
    )(qn, kn, vv, proj, beta, gc, egc, et, gcr, eglr, o_gain.reshape(1, -1))


def _gdn_layer(x, mod, norm_gain, w_in, j, conv_w, a_log, dt_bias, o_gain, w_out):
    shift, scale, gate = mod[0], mod[1], mod[2]
    n_main = GDN_CONV_DIM + GDN_VAL_DIM
    hv = GDN_V_HEADS
    lane_pad = ((0, 0), (0, LANES - hv))
    w_tail = jnp.concatenate([jnp.pad(w_in[j, :, n_main:n_main + hv], lane_pad),
                              jnp.pad(w_in[j, :, n_main + hv:], lane_pad)], axis=1)
    h, tail = _norm(x, norm_gain, scale, shift, w_side=w_tail)
    proj = _matmul(h, w_in, j, n_main)
    qn = _gdn_conv(proj, conv_w, 0, GDN_KEY_DIM, l2=True, scale=HEAD_DIM ** -0.5)
    kn = _gdn_conv(proj, conv_w, GDN_KEY_DIM, GDN_KEY_DIM, l2=True, scale=1.0)
    vv = _gdn_conv(proj, conv_w, 2 * GDN_KEY_DIM, GDN_VAL_DIM, l2=False, scale=1.0)
    gates = _gdn_gates(tail, a_log, dt_bias)
    on = _gdn_chunk(qn, kn, vv, proj, gates, o_gain)
    return _matmul(on, w_out, j, D_MODEL, resid=x, gate=gate, tn=256)


def _gather_rows_kernel(idx_ref, src_ref, o_ref, sem):
    n = o_ref.shape[0]

    def row_copy(r, src_row):
        return pltpu.make_async_copy(src_ref.at[pl.ds(src_row, 1)], o_ref.at[pl.ds(r, 1)], sem)

    def start(r, carry):
        row_copy(r, idx_ref[0, r]).start()
        return carry

    def wait(r, carry):
        row_copy(r, 0).wait()
        return carry

    lax.fori_loop(0, n, start, 0)
    lax.fori_loop(0, n, wait, 0)


def _gather_rows(src, row_index):
    d = src.shape[1]
    n_rows = row_index.shape[0]
    rb = 512
    nb = n_rows // rb
    return pl.pallas_call(
        _gather_rows_kernel,
        out_shape=jax.ShapeDtypeStruct((n_rows, d), src.dtype),
        grid=(nb,),
        in_specs=[pl.BlockSpec((None, 1, rb), lambda i: (i, 0, 0), memory_space=pltpu.SMEM),
                  pl.BlockSpec(memory_space=pl.ANY)],
        out_specs=pl.BlockSpec((rb, d), lambda i: (i, 0)),
        scratch_shapes=[pltpu.SemaphoreType.DMA],
        compiler_params=_cparams(("arbitrary",)),
        name="moe_gather_rows",
    )(row_index.reshape(nb, 1, rb), src)


def _moe_ffn_kernel(te_ref, tv_ref, x_ref, wg_ref, wu_ref, wd_ref, o_ref):
    i = pl.program_id(0)

    @pl.when(tv_ref[i] != 0)
    def _():
        x = x_ref[...].astype(BF16)
        g = _dot(x, wg_ref[...].astype(BF16))
        u = _dot(x, wu_ref[...].astype(BF16))
        a = (_silu(g) * u).astype(BF16)
        o_ref[...] = _dot(a, wd_ref[...].astype(BF16))

    @pl.when(tv_ref[i] == 0)
    def _():
        o_ref[...] = jnp.zeros_like(o_ref)


def _moe_ffn(x_sorted, tile_expert, tile_valid, w_gate, w_up, w_down, layer):
    n_rows, d = x_sorted.shape
    tm = MOE_TILE
    de = w_gate.shape[-1]
    grid_spec = pltpu.PrefetchScalarGridSpec(
        num_scalar_prefetch=2,
        grid=(n_rows // tm,),
        in_specs=[pl.BlockSpec((tm, d), lambda i, te, tv: (i, 0)),
                  pl.BlockSpec((None, None, d, de), lambda i, te, tv: (layer, te[i], 0, 0)),
                  pl.BlockSpec((None, None, d, de), lambda i, te, tv: (layer, te[i], 0, 0)),
                  pl.BlockSpec((None, None, de, d), lambda i, te, tv: (layer, te[i], 0, 0))],
        out_specs=pl.BlockSpec((tm, d), lambda i, te, tv: (i, 0)))
    return pl.pallas_call(
        _moe_ffn_kernel,
        out_shape=jax.ShapeDtypeStruct((n_rows, d), F32),
        grid_spec=grid_spec,
        compiler_params=_cparams(("arbitrary",)),
        name="moe_grouped_ffn",
    )(tile_expert, tile_valid, x_sorted, w_gate, w_up, w_down)


def _moe_combine_kernel(*refs, final_norm):
    if final_norm:
        pa_ref, pb_ref, y_ref, x_ref, rt_ref, gate_ref, nf_ref, o_ref, ba, bb, sem = refs
    else:
        pa_ref, pb_ref, y_ref, x_ref, rt_ref, gate_ref, o_ref, ba, bb, sem = refs
    n = x_ref.shape[0]

    def row_copy(buf, r, src_row):
        return pltpu.make_async_copy(y_ref.at[pl.ds(src_row, 1)], buf.at[pl.ds(r, 1)], sem)

    def start(r, carry):
        row_copy(ba, r, pa_ref[0, r]).start()
        row_copy(bb, r, pb_ref[0, r]).start()
        return carry

    def wait(r, carry):
        row_copy(ba, r, 0).wait()
        row_copy(bb, r, 0).wait()
        return carry

    lax.fori_loop(0, n, start, 0)
    lax.fori_loop(0, n, wait, 0)
    route = rt_ref[...]
    y = route[:, 2:3] * ba[...] + route[:, 3:4] * bb[...]
    out = x_ref[...] + gate_ref[...] * y
    if final_norm:
        out = out * lax.rsqrt(jnp.mean(out * out, axis=-1, keepdims=True) + RMS_EPS) * nf_ref[...]
    o_ref[...] = out


def _moe_combine(y_sorted, pos_a, pos_b, x, route, gate, norm_final=None):
    t, d = x.shape
    tm = 256
    nb = t // tm
    final_norm = norm_final is not None
    idx_spec = pl.BlockSpec((None, 1, tm), lambda i: (i, 0, 0), memory_space=pltpu.SMEM)
    vec_spec = pl.BlockSpec((1, d), lambda i: (0, 0))
    in_specs = [idx_spec, idx_spec, pl.BlockSpec(memory_space=pl.ANY),
                pl.BlockSpec((tm, d), lambda i: (i, 0)),
                pl.BlockSpec((tm, LANES), lambda i: (i, 0)), vec_spec]
    args = [pos_a.reshape(nb, 1, tm), pos_b.reshape(nb, 1, tm), y_sorted, x, route,
            gate.reshape(1, d)]
    if final_norm:
        in_specs.append(vec_spec)
        args.append(norm_final.reshape(1, d))
    return pl.pallas_call(
        functools.partial(_moe_combine_kernel, final_norm=final_norm),
        out_shape=jax.ShapeDtypeStruct((t, d), F32),
        grid=(nb,),
        in_specs=in_specs,
        out_specs=pl.BlockSpec((tm, d), lambda i: (i, 0)),
        scratch_shapes=[pltpu.VMEM((tm, d), F32), pltpu.VMEM((tm, d), F32),
                        pltpu.SemaphoreType.DMA],
        compiler_params=_cparams(("arbitrary",)),
        name="moe_combine",
    )(*args)


def _moe_plan(route):
    t = route.shape[0]
    tm = MOE_TILE
    n_rows = 2 * t + N_EXPERTS * tm
    e = route[:, :2].astype(jnp.int32).reshape(-1)
    onehot = (e[:, None] == jnp.arange(N_EXPERTS, dtype=jnp.int32)[None, :]).astype(jnp.int32)
    csum = jnp.cumsum(onehot, axis=0)
    rank = jnp.take_along_axis(csum - onehot, e[:, None], axis=1)[:, 0]
    counts = csum[-1]
    padded = ((counts + tm - 1) // tm) * tm
    ends = jnp.cumsum(padded)
    pos = (ends - padded)[e] + rank
    row_token = jnp.zeros((n_rows,), jnp.int32).at[pos].set(
        jnp.arange(2 * t, dtype=jnp.int32) // 2)
    tile_start = jnp.arange(n_rows // tm, dtype=jnp.int32) * tm
    tile_expert = jnp.minimum(jnp.searchsorted(ends, tile_start, side="right"),
                              N_EXPERTS - 1).astype(jnp.int32)
    tile_valid = (tile_start < ends[-1]).astype(jnp.int32)
    return row_token, tile_expert, tile_valid, pos[0::2], pos[1::2]


def _moe_layer(x, mod, norm_gain, w_router_pad, router_bias_pad, w_gate, w_up, w_down, layer,
               norm_final=None):
    shift, scale, gate = mod[3], mod[4], mod[5]
    h, route = _norm(x, norm_gain, scale, shift, w_side=w_router_pad, rbias=router_bias_pad,
                     out_dtype=F32)
    row_token, tile_expert, tile_valid, pos_a, pos_b = _moe_plan(route)
    x_sorted = _gather_rows(h, row_token)
    y_sorted = _moe_ffn(x_sorted, tile_expert, tile_valid, w_gate, w_up, w_down, layer)
    return _moe_combine(y_sorted, pos_a, pos_b, x, route, gate, norm_final)


def kernel(x, c, fox_w_in, fox_q_gain, fox_k_gain, fox_b_f, fox_w_o, gdn_w_in, gdn_conv_w, gdn_a_log, gdn_dt_bias, gdn_o_gain, gdn_w_out, ada_w, ada_b, norm_mix, norm_ffn, w_router, router_bias, moe_w_gate, moe_w_up, moe_w_down, norm_final):
    b, s, d = x.shape
    assert b == 1 and d == D_MODEL
    depth = ada_w.shape[0]
    xs = x.reshape(s, d)
    mod_all = _modulation(c, ada_w, ada_b).reshape(depth, 6, d)
    w_router_pad = jnp.pad(w_router, ((0, 0), (0, LANES - N_EXPERTS)))
    router_bias_pad = jnp.pad(router_bias, (0, LANES - N_EXPERTS))
    for i in range(depth):
        j = i // 2
        if i % 2 == 0:
            xs = _fox_layer(xs, mod_all[i], norm_mix[i], fox_w_in, j, fox_q_gain[j],
                            fox_k_gain[j], fox_b_f[j], fox_w_o)
        else:
            xs = _gdn_layer(xs, mod_all[i], norm_mix[i], gdn_w_in, j, gdn_conv_w[j],
                            gdn_a_log[j], gdn_dt_bias[j], gdn_o_gain[j], gdn_w_out)
        xs = _moe_layer(xs, mod_all[i], norm_ffn[i], w_router_pad, router_bias_pad,
                        moe_w_gate, moe_w_up, moe_w_down, i,
                        norm_final if i == depth - 1 else None)
    return xs.reshape(b, s, d)
```
